```python
import math
import jax, jax.numpy as jnp
from jax import lax
import numpy as np

D_MODEL = 1024
BATCH = 4
SEQ = 4096
DEPTH = 4
DEC_BATCH = 128
DEC_SEQ = 4
PAST_LEN = 2048
PAGE_SIZE = 128

N_A_LAYERS = DEPTH // 2
N_B_LAYERS = DEPTH - N_A_LAYERS
EPS = 1e-6
CHUNK = 128
A_WIDTH = 2 * D_MODEL
A_GROUPS = 8
A_GROUP_DIM = A_WIDTH // A_GROUPS
N_HEADS = 16
HEAD_DIM = D_MODEL // N_HEADS
N_KV_HEADS = 4
GQA = N_HEADS // N_KV_HEADS
MOBA_BLOCK = 256
MOBA_TOPK = 3
Q_BLOCK = 64
ROPE_DIM = HEAD_DIM // 4
ROPE_THETA = 500000.0
PEER_HEADS = 8
PEER_N_KEYS = 128
PEER_EXPERTS = PEER_N_KEYS * PEER_N_KEYS
PEER_KEY_DIM = 256
PEER_TOPK = 16
PEER_TOK_BLOCK = 128

kernel_name = 'yoco_gmlp_moba_peer_step'

F32 = jnp.float32


def _rms(x, g):
    xf = x.astype(F32)
    y = xf * lax.rsqrt(jnp.mean(xf * xf, axis=-1, keepdims=True) + EPS)
    return (y * g.astype(F32)).astype(x.dtype)


def _ada(c, w, b):
    m = jax.nn.silu(c) @ w + b
    return jnp.split(m[:, None, :], 6, axis=-1)


def _modnorm(x, g, shift, scale):
    return _rms(x, g) * (1 + scale) + shift


def _rope(x, pos):
    half = ROPE_DIM // 2
    inv = ROPE_THETA ** (-jnp.arange(half, dtype=F32) / half)
    ang = pos.astype(F32)[:, None] * inv[None, :]
    cos = jnp.cos(ang)[:, None, :]
    sin = jnp.sin(ang)[:, None, :]
    xr = x[..., :ROPE_DIM].astype(F32)
    x1, x2 = xr[..., :half], xr[..., half:]
    rot = jnp.concatenate([x1 * cos - x2 * sin, x2 * cos + x1 * sin], axis=-1).astype(x.dtype)
    return jnp.concatenate([rot, x[..., ROPE_DIM:]], axis=-1)


def _chunk_gmlp(h, w_in, b_in, g_sgu, w_s, b_s, w_out):
    bsz, s, _ = h.shape
    z = jax.nn.gelu(h @ w_in + b_in)
    u, v = jnp.split(z, 2, axis=-1)
    v = _rms(v, g_sgu)
    L = min(s, CHUNK)
    nc = s // L
    vg = v.reshape(bsz, nc, L, A_GROUPS, A_GROUP_DIM)
    ws = w_s[:, :L, :L] * jnp.tril(jnp.ones((L, L), w_s.dtype))
    mix = jnp.einsum('gts,bcsgd->bctgd', ws, vg) + b_s[:, :L].T[None, None, :, :, None]
    y = u * mix.reshape(bsz, s, A_WIDTH)
    return y @ w_out, v


def _peer(h, w_pq, sub_keys, peer_u, peer_v):
    shp = h.shape
    x = h.reshape(-1, D_MODEL)
    n = x.shape[0]
    nblk = -(-n // PEER_TOK_BLOCK)
    xp = jnp.pad(x, ((0, nblk * PEER_TOK_BLOCK - n), (0, 0))).reshape(nblk, PEER_TOK_BLOCK, D_MODEL)
    kk = PEER_TOPK

    def step(xb):
        q = (xb @ w_pq).reshape(PEER_TOK_BLOCK, PEER_HEADS, 2, PEER_KEY_DIM // 2)
        s = jnp.einsum('thpd,pkd->thpk', q, sub_keys).astype(F32)
        sv, si = lax.top_k(s, kk)
        cand = sv[:, :, 0, :, None] + sv[:, :, 1, None, :]
        cidx = si[:, :, 0, :, None] * PEER_N_KEYS + si[:, :, 1, None, :]
        fv, fi = lax.top_k(cand.reshape(PEER_TOK_BLOCK, PEER_HEADS, kk * kk), kk)
        e = jnp.take_along_axis(cidx.reshape(PEER_TOK_BLOCK, PEER_HEADS, kk * kk), fi, axis=-1)
        g = jax.nn.softmax(fv, axis=-1)
        a = jax.nn.gelu(jnp.einsum('thkd,td->thk', peer_u[e], xb))
        w = (g * a.astype(F32)).astype(xb.dtype)
        return jnp.einsum('thk,thkd->td', w, peer_v[e])

    y = lax.map(step, xp).reshape(nblk * PEER_TOK_BLOCK, D_MODEL)[:n]
    return y.reshape(shp)


def _shared_kv(x, pos, kv_norm_g, w_kv, k_norm_g):
    bsz, s, _ = x.shape
    kv = _rms(x, kv_norm_g) @ w_kv
    k, v = jnp.split(kv, 2, axis=-1)
    k = k.reshape(bsz, s, N_KV_HEADS, HEAD_DIM)
    v = v.reshape(bsz, s, N_KV_HEADS, HEAD_DIM)
    k = _rope(_rms(k, k_norm_g), pos)
    return k, v


def _queries(h, pos, w_q, q_norm_g):
    bsz, s, _ = h.shape
    q = (h @ w_q).reshape(bsz, s, N_HEADS, HEAD_DIM)
    q = _rope(_rms(q, q_norm_g), pos)
    return q.reshape(bsz, s, N_KV_HEADS, GQA, HEAD_DIM)


def _moba_core(q, k_own, v_own, own_mask, k_sel, v_sel, sel_valid):
    scale = HEAD_DIM ** -0.5
    s_own = jnp.einsum('qkgd,lkd->qkgl', q, k_own).astype(F32) * scale
    s_own = jnp.where(own_mask[:, None, None, :], s_own, -jnp.inf)
    if k_sel is None:
        p = jax.nn.softmax(s_own, axis=-1).astype(v_own.dtype)
        return jnp.einsum('qkgl,lkd->qkgd', p, v_own)
    s_sel = jnp.einsum('qkgd,qkgnd->qkgn', q, k_sel).astype(F32) * scale
    if sel_valid is not None:
        s_sel = jnp.where(sel_valid, s_sel, -jnp.inf)
    L = k_own.shape[0]
    p = jax.nn.softmax(jnp.concatenate([s_own, s_sel], axis=-1), axis=-1).astype(v_own.dtype)
    return (jnp.einsum('qkgl,lkd->qkgd', p[..., :L], v_own)
            + jnp.einsum('qkgn,qkgnd->qkgd', p[..., L:], v_sel))


def _prompt_shared(x, pos, kv_norm_g, w_kv, k_norm_g):
    k, v = _shared_kv(x, pos, kv_norm_g, w_kv, k_norm_g)
    bsz, s = x.shape[:2]
    nb = -(-s // MOBA_BLOCK)
    pad = nb * MOBA_BLOCK - s
    k_pad = jnp.pad(k, ((0, 0), (0, pad), (0, 0), (0, 0)))
    v_pad = jnp.pad(v, ((0, 0), (0, pad), (0, 0), (0, 0)))
    kb = k_pad.reshape(bsz, nb, MOBA_BLOCK, N_KV_HEADS, HEAD_DIM)
    k_means = jnp.mean(kb.astype(F32), axis=2).astype(k.dtype)
    vb = v_pad.reshape(bsz, nb, MOBA_BLOCK, N_KV_HEADS, HEAD_DIM)
    return {'k': k, 'v': v, 'k_pad': k_pad, 'v_pad': v_pad,
            'kb': kb.transpose(0, 3, 1, 2, 4), 'vb': vb.transpose(0, 3, 1, 2, 4),
            'k_means': k_means}


def _moba_prompt(h, pos, w_q, q_norm_g, w_o, sh):
    bsz, s, _ = h.shape
    q = _queries(h, pos, w_q, q_norm_g)
    nb = sh['k_means'].shape[1]
    topk = min(MOBA_TOPK, nb - 1)
    nqb = s // Q_BLOCK

    def blockify(a):
        return a.reshape((bsz * nqb, Q_BLOCK) + a.shape[2:])

    xs = {'q': blockify(q),
          'b': jnp.repeat(jnp.arange(bsz, dtype=jnp.int32), nqb),
          'q0': jnp.tile(jnp.arange(nqb, dtype=jnp.int32) * Q_BLOCK, bsz)}
    if topk > 0:
        q_blk = pos // MOBA_BLOCK
        gate = jnp.einsum('bskgd,bnkd->bskgn', q, sh['k_means']).astype(F32)
        is_past = jnp.arange(nb)[None, :] < q_blk[:, None]
        gate = jnp.where(is_past[None, :, None, None, :], gate, -jnp.inf)
        _, idx = lax.top_k(gate, topk)
        xs['idx'] = blockify(idx)
        xs['valid'] = blockify(idx < q_blk[None, :, None, None, None])
    kvh = jnp.arange(N_KV_HEADS)[None, :, None, None]

    def step(blk):
        b, q0 = blk['b'], blk['q0']
        own0 = (q0 // MOBA_BLOCK) * MOBA_BLOCK
        k_own = lax.dynamic_slice(sh['k_pad'], (b, own0, 0, 0), (1, MOBA_BLOCK, N_KV_HEADS, HEAD_DIM))[0]
        v_own = lax.dynamic_slice(sh['v_pad'], (b, own0, 0, 0), (1, MOBA_BLOCK, N_KV_HEADS, HEAD_DIM))[0]
        own_mask = (own0 + jnp.arange(MOBA_BLOCK))[None, :] <= (q0 + jnp.arange(Q_BLOCK))[:, None]
        if topk > 0:
            sel_shape = (Q_BLOCK, N_KV_HEADS, GQA, topk * MOBA_BLOCK, HEAD_DIM)
            k_sel = sh['kb'][b, kvh, blk['idx']].reshape(sel_shape)
            v_sel = sh['vb'][b, kvh, blk['idx']].reshape(sel_shape)
            valid = jnp.repeat(blk['valid'], MOBA_BLOCK, axis=-1)
            return _moba_core(blk['q'], k_own, v_own, own_mask, k_sel, v_sel, valid)
        return _moba_core(blk['q'], k_own, v_own, own_mask, None, None, None)

    o = lax.map(step, xs)
    return o.reshape(bsz, s, N_HEADS * HEAD_DIM) @ w_o


def _sample_shared(x, pos, kv_norm_g, w_kv, k_norm_g, cache_k, cache_v, page_table):
    k_new, v_new = _shared_kv(x, pos, kv_norm_g, w_kv, k_norm_g)
    dbsz = x.shape[0]
    n_pages = page_table.shape[1]
    ppb = MOBA_BLOCK // PAGE_SIZE
    nbp = (n_pages * PAGE_SIZE) // MOBA_BLOCK
    full = nbp * ppb
    rem_rows = (n_pages - full) * PAGE_SIZE

    def rows(cache, pages):
        g = cache[pages]
        return g.transpose(0, 1, 3, 2, 4).reshape(dbsz, rem_rows, N_KV_HEADS, HEAD_DIM)

    out = {'k_new': k_new, 'v_new': v_new, 'nbp': nbp,
           'k_own': jnp.concatenate([rows(cache_k, page_table[:, full:]), k_new], axis=1),
           'v_own': jnp.concatenate([rows(cache_v, page_table[:, full:]), v_new], axis=1)}
    if nbp > 0:
        pm = jnp.mean(cache_k[page_table[:, :full]].astype(F32), axis=3)
        out['k_means'] = pm.reshape(dbsz, nbp, ppb, N_KV_HEADS, HEAD_DIM).mean(axis=2).astype(cache_k.dtype)
    return out


def _moba_sample(h, pos, w_q, q_norm_g, w_o, sh, cache_k, cache_v, page_table):
    dbsz, t, _ = h.shape
    q = _queries(h, pos, w_q, q_norm_g)
    topk = min(MOBA_TOPK, sh['nbp'])
    rem = sh['k_own'].shape[1] - t
    own_mask = jnp.concatenate([jnp.ones((t, rem), bool), jnp.tril(jnp.ones((t, t), bool))], axis=1)
    xs = {'q': q, 'k_own': sh['k_own'], 'v_own': sh['v_own'], 'prow': page_table}
    if topk > 0:
        gate = jnp.einsum('btkgd,bnkd->btkgn', q, sh['k_means']).astype(F32)
        _, xs['idx'] = lax.top_k(gate, topk)
    ppb = MOBA_BLOCK // PAGE_SIZE
    kvh = jnp.arange(N_KV_HEADS)[None, :, None, None, None]

    def step(seq):
        if topk > 0:
            pt = seq['prow'][seq['idx'][..., None] * ppb + jnp.arange(ppb)]
            sel_shape = (t, N_KV_HEADS, GQA, topk * MOBA_BLOCK, HEAD_DIM)
            k_sel = cache_k[pt, kvh].reshape(sel_shape)
            v_sel = cache_v[pt, kvh].reshape(sel_shape)
            return _moba_core(seq['q'], seq['k_own'], seq['v_own'], own_mask, k_sel, v_sel, None)
        return _moba_core(seq['q'], seq['k_own'], seq['v_own'], own_mask, None, None, None)

    o = lax.map(step, xs)
    return o.reshape(dbsz, t, N_HEADS * HEAD_DIM) @ w_o


def setup_inputs(seed: int = 0) -> dict:
    key = jax.random.key(seed)
    ks = jax.random.split(key, 28)

    def nrm(k, shape, scale=1.0):
        return jax.random.normal(k, shape, F32) * scale

    def gain(k, shape):
        return 1.0 + 0.02 * jax.random.normal(k, shape, F32)

    n_pages = PAST_LEN // PAGE_SIZE
    used = DEC_BATCH * n_pages
    n_phys = used + (used + 3) // 4
    perm = jax.random.permutation(ks[0], n_phys).astype(jnp.int32)
    kvw = 2 * N_KV_HEADS * HEAD_DIM
    qw = N_HEADS * HEAD_DIM
    return {
        'x_prompt': nrm(ks[1], (BATCH, SEQ, D_MODEL)),
        'x_sample': nrm(ks[2], (DEC_BATCH, DEC_SEQ, D_MODEL)),
        'cache_k': nrm(ks[3], (n_phys, N_KV_HEADS, PAGE_SIZE, HEAD_DIM)),
        'cache_v': nrm(ks[4], (n_phys, N_KV_HEADS, PAGE_SIZE, HEAD_DIM)),
        'page_table': perm[:used].reshape(DEC_BATCH, n_pages),
        'c_prompt': nrm(ks[5], (BATCH, D_MODEL)),
        'c_sample': nrm(ks[6], (DEC_BATCH, D_MODEL)),
        'ada_w': nrm(ks[7], (DEPTH, D_MODEL, 6 * D_MODEL), 0.5 * D_MODEL ** -0.5),
        'ada_b': nrm(ks[8], (DEPTH, 6 * D_MODEL), 0.02),
        'norm1_g': gain(ks[9], (DEPTH, D_MODEL)),
        'norm2_g': gain(ks[10], (DEPTH, D_MODEL)),
        'a_w_in': nrm(ks[11], (N_A_LAYERS, D_MODEL, 2 * A_WIDTH), D_MODEL ** -0.5),
        'a_b_in': nrm(ks[12], (N_A_LAYERS, 2 * A_WIDTH), 0.02),
        'a_g_sgu': gain(ks[13], (N_A_LAYERS, A_WIDTH)),
        'a_w_s': nrm(ks[14], (N_A_LAYERS, A_GROUPS, CHUNK, CHUNK), CHUNK ** -0.5),
        'a_b_s': nrm(ks[15], (N_A_LAYERS, A_GROUPS, CHUNK), 0.02),
        'a_w_out': nrm(ks[16], (N_A_LAYERS, A_WIDTH, D_MODEL), A_WIDTH ** -0.5),
        'kv_norm_g': gain(ks[17], (D_MODEL,)),
        'w_kv': nrm(ks[18], (D_MODEL, kvw), D_MODEL ** -0.5),
        'k_norm_g': gain(ks[19], (HEAD_DIM,)),
        'b_w_q': nrm(ks[20], (N_B_LAYERS, D_MODEL, qw), D_MODEL ** -0.5),
        'b_q_norm_g': gain(ks[21], (N_B_LAYERS, HEAD_DIM)),
        'b_w_o': nrm(ks[22], (N_B_LAYERS, qw, D_MODEL), qw ** -0.5),
        'peer_w_q': nrm(ks[23], (DEPTH, D_MODEL, PEER_HEADS * PEER_KEY_DIM), D_MODEL ** -0.5),
        'peer_sub_keys': nrm(ks[24], (DEPTH, 2, PEER_N_KEYS, PEER_KEY_DIM // 2), (PEER_KEY_DIM // 2) ** -0.5),
        'peer_u': nrm(ks[25], (DEPTH, PEER_EXPERTS, D_MODEL), D_MODEL ** -0.5),
        'peer_v': nrm(ks[26], (DEPTH, PEER_EXPERTS, D_MODEL), PEER_HEADS ** -0.5),
    }


def reference(x_prompt, x_sample, cache_k, cache_v, page_table, c_prompt, c_sample,
              ada_w, ada_b, norm1_g, norm2_g,
              a_w_in, a_b_in, a_g_sgu, a_w_s, a_b_s, a_w_out,
              kv_norm_g, w_kv, k_norm_g,
              b_w_q, b_q_norm_g, b_w_o,
              peer_w_q, peer_sub_keys, peer_u, peer_v):
    def trunk(x, c, make_shared, attend):
        a_rows = []
        shared = None
        for l in range(DEPTH):
            shift1, scale1, gate1, shift2, scale2, gate2 = _ada(c, ada_w[l], ada_b[l])
            h = _modnorm(x, norm1_g[l], shift1, scale1)
            if l < N_A_LAYERS:
                m, v_rows = _chunk_gmlp(h, a_w_in[l], a_b_in[l], a_g_sgu[l], a_w_s[l], a_b_s[l], a_w_out[l])
                a_rows.append(v_rows)
            else:
                j = l - N_A_LAYERS
                m = attend(h, b_w_q[j], b_q_norm_g[j], b_w_o[j], shared)
            x = x + gate1 * m
            h = _modnorm(x, norm2_g[l], shift2, scale2)
            x = x + gate2 * _peer(h, peer_w_q[l], peer_sub_keys[l], peer_u[l], peer_v[l])
            if l == N_A_LAYERS - 1:
                shared = make_shared(x)
        return x, a_rows, shared

    bsz, s_len, _ = x_prompt.shape
    pos_p = jnp.arange(s_len, dtype=jnp.int32)
    y_prompt, _, sp = trunk(
        x_prompt, c_prompt,
        lambda x: _prompt_shared(x, pos_p, kv_norm_g, w_kv, k_norm_g),
        lambda h, wq, gq, wo, sh: _moba_prompt(h, pos_p, wq, gq, wo, sh))

    dbsz, t_len, _ = x_sample.shape
    past_len = page_table.shape[1] * PAGE_SIZE
    pos_s = past_len + jnp.arange(t_len, dtype=jnp.int32)
    y_sample, a_rows_s, ss = trunk(
        x_sample, c_sample,
        lambda x: _sample_shared(x, pos_s, kv_norm_g, w_kv, k_norm_g, cache_k, cache_v, page_table),
        lambda h, wq, gq, wo, sh: _moba_sample(h, pos_s, wq, gq, wo, sh, cache_k, cache_v, page_table))

    n_pp = s_len // PAGE_SIZE
    k_prompt = sp['k'].reshape(bsz, n_pp, PAGE_SIZE, N_KV_HEADS, HEAD_DIM).transpose(0, 1, 3, 2, 4)
    v_prompt = sp['v'].reshape(bsz, n_pp, PAGE_SIZE, N_KV_HEADS, HEAD_DIM).transpose(0, 1, 3, 2, 4)
    k_sample = ss['k_new'].transpose(0, 2, 1, 3)
    v_sample = ss['v_new'].transpose(0, 2, 1, 3)
    a_v_sample = jnp.stack(a_rows_s, axis=0)
    return (y_prompt, y_sample, k_prompt, v_prompt, k_sample, v_sample, a_v_sample)
```

```python
import functools
import math

import jax
import jax.numpy as jnp
from jax import lax
from jax.experimental import pallas as pl
from jax.experimental.pallas import tpu as pltpu

F32 = jnp.float32
BF16 = jnp.bfloat16

EPS = 1e-6
MOBA_BLOCK = 256
MOBA_TOPK = 3
ROPE_THETA = 500000.0
PEER_TOPK = 16
NEG = -1e30

V7X_VMEM_BYTES = 64 * 1024 * 1024
VMEM_LIMIT = 56 * 1024 * 1024
LANES = 128

TOK_TILE = 256
PEER_TOK_TILE = 512
PEER_EXPERT_CHUNK = 1024

_NT = (((1,), (1,)), ((), ()))


def _cparams(sem):
    return pltpu.CompilerParams(dimension_semantics=sem, vmem_limit_bytes=VMEM_LIMIT)


def _gelu(x):
    c = math.sqrt(2.0 / math.pi)
    return 0.5 * x * (1.0 + jnp.tanh(c * (x + 0.044715 * (x * x * x))))


def _modnorm(x, g, shift, scale):
    ms = jnp.mean(x * x, axis=-1, keepdims=True)
    return (x * lax.rsqrt(ms + EPS) * g) * (1.0 + scale) + shift


def _head_norm_rope_t(blk, gain, cos, sin, half):
    ms = jnp.mean(blk * blk, axis=0, keepdims=True)
    blk = blk * lax.rsqrt(ms + EPS) * gain
    x1 = blk[0:half]
    x2 = blk[half:2 * half]
    return jnp.concatenate([x1 * cos - x2 * sin, x2 * cos + x1 * sin, blk[2 * half:]], axis=0)


def _ada_kernel(c_ref, w_ref, b_ref, o_ref):
    c = c_ref[...]
    s = c * (1.0 / (1.0 + jnp.exp(-c)))
    o_ref[0] = jnp.dot(s, w_ref[0], preferred_element_type=F32) + b_ref[0]


def _ada_call(c_all, ada_w, ada_b):
    depth, d, n = ada_w.shape
    m = c_all.shape[0]
    tn = 1024
    return pl.pallas_call(
        _ada_kernel,
        out_shape=jax.ShapeDtypeStruct((depth, m, n), F32),
        grid=(depth, n // tn),
        in_specs=[pl.BlockSpec((m, d), lambda l, j: (0, 0)),
                  pl.BlockSpec((1, d, tn), lambda l, j: (l, 0, j)),
                  pl.BlockSpec((1, 1, tn), lambda l, j: (l, 0, j))],
        out_specs=pl.BlockSpec((1, m, tn), lambda l, j: (l, 0, j)),
        compiler_params=_cparams(("arbitrary", "arbitrary")),
        name="ada_mod",
    )(c_all, ada_w, ada_b.reshape(depth, 1, n))


def _gmlp_kernel(x_ref, mod_ref, g1_ref, win_ref, bin_ref, gs_ref, wmix_ref, bmix_ref, wout_ref,
                 *rest, chunk, groups, emit_v):
    if emit_v:
        o_ref, v_ref, vb_scr, y_scr = rest
    else:
        o_ref, vb_scr, y_scr = rest
    x = x_ref[...]
    tt, d = x.shape
    w = vb_scr.shape[1]
    gd = w // groups
    mod = mod_ref[0]
    h = _modnorm(x, g1_ref[...], mod[:, 0:d], mod[:, d:2 * d]).astype(BF16)
    v = _gelu(jnp.dot(h, win_ref[:, w:], preferred_element_type=F32) + bin_ref[:, w:])
    v = v * lax.rsqrt(jnp.mean(v * v, axis=-1, keepdims=True) + EPS) * gs_ref[...]
    if emit_v:
        v_ref[...] = v
    vb_scr[...] = v.astype(BF16)
    for g in range(groups):
        cols = slice(g * gd, (g + 1) * gd)
        u = _gelu(jnp.dot(h, win_ref[:, cols], preferred_element_type=F32) + bin_ref[:, cols])
        wm = wmix_ref[g]
        for c in range(tt // chunk):
            rows = slice(c * chunk, (c + 1) * chunk)
            mix = jnp.dot(wm, vb_scr[rows, cols], preferred_element_type=F32) + bmix_ref[g]
            y_scr[rows, cols] = (u[rows] * mix).astype(BF16)
    m = jnp.dot(y_scr[...], wout_ref[...], preferred_element_type=F32)
    o_ref[...] = x + mod[:, 2 * d:3 * d] * m


def _gmlp_call(x, mod, mod_map, g1, w_in, b_in, g_sgu, wmix, bmix, w_out, *, chunk, emit_v):
    t, d = x.shape
    w2 = w_in.shape[1]
    w = w2 // 2
    groups = wmix.shape[0]
    tt = TOK_TILE
    out_shape = [jax.ShapeDtypeStruct((t, d), F32)]
    out_specs = [pl.BlockSpec((tt, d), lambda i: (i, 0))]
    if emit_v:
        out_shape.append(jax.ShapeDtypeStruct((t, w), F32))
        out_specs.append(pl.BlockSpec((tt, w), lambda i: (i, 0)))
    res = pl.pallas_call(
        functools.partial(_gmlp_kernel, chunk=chunk, groups=groups, emit_v=emit_v),
        out_shape=out_shape,
        grid=(t // tt,),
        in_specs=[pl.BlockSpec((tt, d), lambda i: (i, 0)),
                  pl.BlockSpec((1,) + mod.shape[1:], mod_map),
                  pl.BlockSpec((1, d), lambda i: (0, 0)),
                  pl.BlockSpec((d, w2), lambda i: (0, 0)),
                  pl.BlockSpec((1, w2), lambda i: (0, 0)),
                  pl.BlockSpec((1, w), lambda i: (0, 0)),
                  pl.BlockSpec(wmix.shape, lambda i: (0, 0, 0)),
                  pl.BlockSpec(bmix.shape, lambda i: (0, 0, 0)),
                  pl.BlockSpec((w, d), lambda i: (0, 0))],
        out_specs=out_specs,
        scratch_shapes=[pltpu.VMEM((tt, w), BF16), pltpu.VMEM((tt, w), BF16)],
        compiler_params=_cparams(("arbitrary",)),
        name="gmlp_mixer",
    )(x, mod, g1, w_in, b_in, g_sgu, wmix, bmix, w_out)
    return res if emit_v else (res[0], None)


def _peer_select_kernel(x_ref, mod_ref, g2_ref, wpq_t_ref, keys_ref,
                        hb_ref, ns0_ref, e0_ref, d_ref, e1_ref, q_scr, *, n_heads, topk):
    x = x_ref[...]
    d = x.shape[1]
    mod = mod_ref[0]
    hb = _modnorm(x, g2_ref[...], mod[:, 3 * d:4 * d], mod[:, 4 * d:5 * d]).astype(BF16)
    hb_ref[...] = hb
    q_scr[...] = lax.dot_general(wpq_t_ref[...], hb, _NT, preferred_element_type=F32)
    dk = keys_ref.shape[2]

    def top_values(s, n):
        vals = []
        cur = s
        for a in range(n):
            m = jnp.max(cur, axis=0, keepdims=True)
            vals.append(m)
            if a < n - 1:
                cur = jnp.where(cur == m, -jnp.inf, cur)
        return vals

    def head(hd, carry):
        r0 = pl.multiple_of(hd * (2 * dk), 2 * dk)
        s0 = jnp.dot(keys_ref[0], q_scr[pl.ds(r0, dk), :], preferred_element_type=F32)
        s1 = jnp.dot(keys_ref[1], q_scr[pl.ds(r0 + dk, dk), :], preferred_element_type=F32)
        p0 = top_values(s0, topk)
        p1 = jnp.concatenate(top_values(s1, topk), axis=0)
        cand = jnp.concatenate([p0[a] + p1 for a in range(topk)], axis=0)
        ms = top_values(cand, topk + 1)
        z = jnp.ones_like(ms[0])
        for mk in ms[1:topk]:
            z = z + jnp.exp(mk - ms[0])
        tau = 0.5 * (ms[topk - 1] + ms[topk])
        ns0_ref[hd] = -s0
        e0_ref[hd] = jnp.exp(s0 - p0[0]) * (1.0 / z)
        d_ref[hd] = s1 - tau
        e1_ref[hd] = jnp.exp(s1 - p1[0:1])
        return carry

    lax.fori_loop(0, n_heads, head, 0)


def _peer_select_call(x, mod, mod_map, g2, wpq_t, sub_keys):
    t, d = x.shape
    nq = wpq_t.shape[0]
    _, n_keys, dk = sub_keys.shape
    n_heads = nq // (2 * dk)
    ts = TOK_TILE
    tab = jax.ShapeDtypeStruct((n_heads, n_keys, t), F32)
    tab_spec = pl.BlockSpec((n_heads, n_keys, ts), lambda i: (0, 0, i))
    return pl.pallas_call(
        functools.partial(_peer_select_kernel, n_heads=n_heads, topk=PEER_TOPK),
        out_shape=[jax.ShapeDtypeStruct((t, d), BF16), tab, tab, tab, tab],
        grid=(t // ts,),
        in_specs=[pl.BlockSpec((ts, d), lambda i: (i, 0)),
                  pl.BlockSpec((1,) + mod.shape[1:], mod_map),
                  pl.BlockSpec((1, d), lambda i: (0, 0)),
                  pl.BlockSpec((nq, d), lambda i: (0, 0)),
                  pl.BlockSpec(sub_keys.shape, lambda i: (0, 0, 0))],
        out_specs=[pl.BlockSpec((ts, d), lambda i: (i, 0)), tab_spec, tab_spec, tab_spec, tab_spec],
        scratch_shapes=[pltpu.VMEM((nq, ts), F32)],
        compiler_params=_cparams(("arbitrary",)),
        name="peer_select",
    )(x, mod, g2, wpq_t, sub_keys)


def _peer_dense_kernel(hb_ref, u_ref, vt_ref, ns0_ref, e0_ref, d_ref, e1_ref, x_ref, mod_ref,
                       o_ref, acc_ref, at_ref, wt_ref, *, n_heads, n_keys):
    j = pl.program_id(1)
    tb, d = x_ref.shape
    ec = u_ref.shape[0]

    @pl.when(j == 0)
    def _():
        acc_ref[...] = jnp.zeros_like(acc_ref)

    at_ref[...] = lax.dot_general(u_ref[...], hb_ref[...], _NT, preferred_element_type=F32)

    sub = 16
    for r in range(ec // n_keys):
        for ts in range(tb // LANES):
            lanes = slice(ts * LANES, (ts + 1) * LANES)
            ns = [ns0_ref[hd, r:r + 1, lanes] for hd in range(n_heads)]
            e0 = [e0_ref[hd, r:r + 1, lanes] for hd in range(n_heads)]

            def body(jg, carry, r=r, lanes=lanes, ns=ns, e0=e0):
                j0 = pl.multiple_of(jg * sub, sub)
                c = jnp.zeros((sub, LANES), F32)
                for hd in range(n_heads):
                    dd = d_ref[hd, pl.ds(j0, sub), lanes]
                    ee = e1_ref[hd, pl.ds(j0, sub), lanes]
                    c = c + jnp.where(dd >= ns[hd], ee * e0[hd], 0.0)
                rows = pl.ds(r * n_keys + j0, sub)
                wt_ref[rows, lanes] = (c * _gelu(at_ref[rows, lanes])).astype(BF16)
                return carry

            lax.fori_loop(0, n_keys // sub, body, 0)

    acc_ref[...] += jnp.dot(vt_ref[...], wt_ref[...], preferred_element_type=F32)

    @pl.when(j == pl.num_programs(1) - 1)
    def _():
        mod = mod_ref[0]
        o_ref[...] = x_ref[...] + mod[:, 5 * d:6 * d] * acc_ref[...].T


def _peer_dense_call(x, hb, mod, mod_map2, u_b, vt_b, ns0, e0, dd, e1):
    t, d = x.shape
    n_exp = u_b.shape[0]
    n_heads, n_keys, _ = ns0.shape
    tb = min(PEER_TOK_TILE, t)
    ec = min(PEER_EXPERT_CHUNK, n_exp)
    rpc = ec // n_keys
    return pl.pallas_call(
        functools.partial(_peer_dense_kernel, n_heads=n_heads, n_keys=n_keys),
        out_shape=jax.ShapeDtypeStruct((t, d), F32),
        grid=(t // tb, n_exp // ec),
        in_specs=[pl.BlockSpec((tb, d), lambda i, j: (i, 0)),
                  pl.BlockSpec((ec, d), lambda i, j: (j, 0)),
                  pl.BlockSpec((d, ec), lambda i, j: (0, j)),
                  pl.BlockSpec((n_heads, rpc, tb), lambda i, j: (0, j, i)),
                  pl.BlockSpec((n_heads, rpc, tb), lambda i, j: (0, j, i)),
                  pl.BlockSpec((n_heads, n_keys, tb), lambda i, j: (0, 0, i)),
                  pl.BlockSpec((n_heads, n_keys, tb), lambda i, j: (0, 0, i)),
                  pl.BlockSpec((tb, d), lambda i, j: (i, 0)),
                  pl.BlockSpec((1,) + mod.shape[1:], mod_map2)],
        out_specs=pl.BlockSpec((tb, d), lambda i, j: (i, 0)),
        scratch_shapes=[pltpu.VMEM((d, tb), F32), pltpu.VMEM((ec, tb), F32), pltpu.VMEM((ec, tb), BF16)],
        compiler_params=_cparams(("arbitrary", "arbitrary")),
        name="peer_dense",
    )(hb, u_b, vt_b, ns0, e0, dd, e1, x, mod)


def _kv_kernel(x_ref, gkv_ref, wkv_t_ref, gk_ref, cos_ref, sin_ref, kt_ref, vt_ref, km_ref, *, n_kv, hd):
    x = x_ref[...]
    tk = x.shape[0]
    xn = x * lax.rsqrt(jnp.mean(x * x, axis=-1, keepdims=True) + EPS) * gkv_ref[...]
    kvt = lax.dot_general(wkv_t_ref[...], xn.astype(BF16), _NT, preferred_element_type=F32)
    cos = cos_ref[...]
    sin = sin_ref[...]
    half = cos.shape[0]
    for h in range(n_kv):
        kt_ref[h * hd:(h + 1) * hd, :] = _head_norm_rope_t(kvt[h * hd:(h + 1) * hd], gk_ref[...], cos, sin, half)
    vt_ref[...] = kvt[n_kv * hd:]
    kt = kt_ref[...]
    hi = kt.astype(BF16)
    lo = (kt - hi.astype(F32)).astype(BF16)
    ones = jnp.ones((8, tk), BF16)
    s = (lax.dot_general(ones, hi, _NT, preferred_element_type=F32)
         + lax.dot_general(ones, lo, _NT, preferred_element_type=F32))
    km_ref[0] = s * (1.0 / tk)


def _kv_call(x, gkv, wkv_t, gk_b, cos_t, sin_t, *, n_kv, hd):
    t, d = x.shape
    tk = TOK_TILE
    assert tk == MOBA_BLOCK
    nkv = n_kv * hd
    half = cos_t.shape[0]
    return pl.pallas_call(
        functools.partial(_kv_kernel, n_kv=n_kv, hd=hd),
        out_shape=[jax.ShapeDtypeStruct((nkv, t), F32), jax.ShapeDtypeStruct((nkv, t), F32),
                   jax.ShapeDtypeStruct((t // tk, 8, nkv), F32)],
        grid=(t // tk,),
        in_specs=[pl.BlockSpec((tk, d), lambda i: (i, 0)),
                  pl.BlockSpec((1, d), lambda i: (0, 0)),
                  pl.BlockSpec((2 * nkv, d), lambda i: (0, 0)),
                  pl.BlockSpec((hd, tk), lambda i: (0, 0)),
                  pl.BlockSpec((half, tk), lambda i: (0, i)),
                  pl.BlockSpec((half, tk), lambda i: (0, i))],
        out_specs=[pl.BlockSpec((nkv, tk), lambda i: (0, i)), pl.BlockSpec((nkv, tk), lambda i: (0, i)),
                   pl.BlockSpec((1, 8, nkv), lambda i: (i, 0, 0))],
        compiler_params=_cparams(("arbitrary",)),
        name="shared_kv",
    )(x, gkv, wkv_t, gk_b, cos_t, sin_t)


def _qproj_kernel(x_ref, mod_ref, g1_ref, wq_t_ref, gq_ref, cos_ref, sin_ref, qt_ref, *, n_heads, hd):
    x = x_ref[...]
    d = x.shape[1]
    mod = mod_ref[0]
    h = _modnorm(x, g1_ref[...], mod[:, 0:d], mod[:, d:2 * d]).astype(BF16)
    qt = lax.dot_general(wq_t_ref[...], h, _NT, preferred_element_type=F32)
    cos = cos_ref[...]
    sin = sin_ref[...]
    half = cos.shape[0]
    for hh in range(n_heads):
        blk = _head_norm_rope_t(qt[hh * hd:(hh + 1) * hd], gq_ref[...], cos, sin, half)
        qt_ref[hh * hd:(hh + 1) * hd, :] = blk.astype(BF16)


def _qproj_call(x, mod, mod_map, g1, wq_t, gq_b, cos_t, sin_t, *, hd):
    t, d = x.shape
    nq = wq_t.shape[0]
    tq = TOK_TILE
    half = cos_t.shape[0]
    return pl.pallas_call(
        functools.partial(_qproj_kernel, n_heads=nq // hd, hd=hd),
        out_shape=jax.ShapeDtypeStruct((nq, t), BF16),
        grid=(t // tq,),
        in_specs=[pl.BlockSpec((tq, d), lambda i: (i, 0)),
                  pl.BlockSpec((1,) + mod.shape[1:], mod_map),
                  pl.BlockSpec((1, d), lambda i: (0, 0)),
                  pl.BlockSpec((nq, d), lambda i: (0, 0)),
                  pl.BlockSpec((hd, tq), lambda i: (0, 0)),
                  pl.BlockSpec((half, tq), lambda i: (0, i)),
                  pl.BlockSpec((half, tq), lambda i: (0, i))],
        out_specs=pl.BlockSpec((nq, tq), lambda i: (0, i)),
        compiler_params=_cparams(("arbitrary",)),
        name="moba_qproj",
    )(x, mod, g1, wq_t, gq_b, cos_t, sin_t)


def _oproj_kernel(ot_ref, wo_t_ref, x_ref, mod_ref, o_ref):
    d = x_ref.shape[1]
    mt = jnp.dot(wo_t_ref[...], ot_ref[...], preferred_element_type=F32)
    mod = mod_ref[0]
    o_ref[...] = x_ref[...] + mod[:, 2 * d:3 * d] * mt.T


def _oproj_call(ot, wo_t, x, mod, mod_map):
    t, d = x.shape
    nq = ot.shape[0]
    tq = TOK_TILE
    return pl.pallas_call(
        _oproj_kernel,
        out_shape=jax.ShapeDtypeStruct((t, d), F32),
        grid=(t // tq,),
        in_specs=[pl.BlockSpec((nq, tq), lambda i: (0, i)),
                  pl.BlockSpec((d, nq), lambda i: (0, 0)),
                  pl.BlockSpec((tq, d), lambda i: (i, 0)),
                  pl.BlockSpec((1,) + mod.shape[1:], mod_map)],
        out_specs=pl.BlockSpec((tq, d), lambda i: (i, 0)),
        compiler_params=_cparams(("arbitrary",)),
        name="moba_oproj",
    )(ot, wo_t, x, mod)


def _moba_prompt_kernel(qt_ref, kb_ref, vt_ref, km_ref, ot_ref, bias_scr, m_scr, l_scr, acc_scr,
                        *, n_kv, gqa, hd, topk):
    qi = pl.program_id(1)
    tq = qt_ref.shape[1]
    nb = kb_ref.shape[2]
    blk = kb_ref.shape[3]
    gt = gqa * tq
    scale = hd ** -0.5
    rr = lax.broadcasted_iota(jnp.int32, (blk, tq), 0)
    tt = lax.broadcasted_iota(jnp.int32, (blk, tq), 1)
    causal = jnp.where(rr <= tt, 0.0, NEG).astype(F32)
    causal = jnp.concatenate([causal] * gqa, axis=1)
    bidx = lax.broadcasted_iota(jnp.int32, (nb, gt), 0)

    def update(st, v_blk):
        m_old = m_scr[...]
        m_new = jnp.maximum(m_old, jnp.max(st, axis=0, keepdims=True))
        p = jnp.exp(st - m_new)
        alpha = jnp.exp(m_old - m_new)
        l_scr[...] = alpha * l_scr[...] + jnp.sum(p, axis=0, keepdims=True)
        acc_scr[...] = alpha * acc_scr[...] + jnp.dot(v_blk, p.astype(BF16), preferred_element_type=F32)
        m_scr[...] = m_new

    for kh in range(n_kv):
        q4 = jnp.concatenate([qt_ref[(kh * gqa + g) * hd:(kh * gqa + g + 1) * hd, :] for g in range(gqa)], axis=1)
        gate = jnp.dot(km_ref[0, kh], q4, preferred_element_type=F32)
        past = bidx < qi
        gm = jnp.where(past, gate, -jnp.inf)
        cur = gm
        thr = None
        for a in range(topk):
            thr = jnp.max(cur, axis=0, keepdims=True)
            if a < topk - 1:
                cur = jnp.where(cur == thr, -jnp.inf, cur)
        bias_scr[...] = jnp.where(jnp.logical_and(past, gm >= thr), 0.0, NEG)
        m_scr[...] = jnp.full_like(m_scr, NEG)
        l_scr[...] = jnp.zeros_like(l_scr)
        acc_scr[...] = jnp.zeros_like(acc_scr)

        def body(n, carry, kh=kh, q4=q4):
            st = jnp.dot(kb_ref[0, kh, n], q4, preferred_element_type=F32) * scale + bias_scr[pl.ds(n, 1), :]
            update(st, vt_ref[0, kh, n])
            return carry

        lax.fori_loop(0, qi, body, 0)
        st = jnp.dot(kb_ref[0, kh, qi], q4, preferred_element_type=F32) * scale + causal
        update(st, vt_ref[0, kh, qi])
        o = acc_scr[...] * (1.0 / l_scr[...])
        for g in range(gqa):
            ot_ref[(kh * gqa + g) * hd:(kh * gqa + g + 1) * hd, :] = o[:, g * tq:(g + 1) * tq].astype(BF16)


def _moba_prompt_call(qt, kb, vt, km, *, gqa):
    nq, t = qt.shape
    bsz, n_kv, nb, blk, hd = kb.shape
    tq = TOK_TILE
    assert tq == blk == MOBA_BLOCK
    topk = min(MOBA_TOPK, nb - 1)
    assert topk > 0
    gt = gqa * tq
    return pl.pallas_call(
        functools.partial(_moba_prompt_kernel, n_kv=n_kv, gqa=gqa, hd=hd, topk=topk),
        out_shape=jax.ShapeDtypeStruct((nq, t), BF16),
        grid=(bsz, nb),
        in_specs=[pl.BlockSpec((nq, tq), lambda b, i: (0, b * nb + i)),
                  pl.BlockSpec((1, n_kv, nb, blk, hd), lambda b, i: (b, 0, 0, 0, 0)),
                  pl.BlockSpec((1, n_kv, nb, hd, blk), lambda b, i: (b, 0, 0, 0, 0)),
                  pl.BlockSpec((1, n_kv, nb, hd), lambda b, i: (b, 0, 0, 0))],
        out_specs=pl.BlockSpec((nq, tq), lambda b, i: (0, b * nb + i)),
        scratch_shapes=[pltpu.VMEM((nb, gt), F32), pltpu.VMEM((1, gt), F32), pltpu.VMEM((1, gt), F32),
                        pltpu.VMEM((hd, gt), F32)],
        compiler_params=_cparams(("arbitrary", "arbitrary")),
        name="moba_prompt",
    )(qt, kb, vt, km)


def _moba_sample_kernel(pt_ref, q_ref, kn_ref, vn_ref, *rest, n_pages, n_kv, t_len, topk):
    kp = rest[:n_pages]
    vp = rest[n_pages:2 * n_pages]
    o_ref = rest[2 * n_pages]
    page = kp[0].shape[2]
    hd = kp[0].shape[3]
    ppb = MOBA_BLOCK // page
    nbp = n_pages // ppb
    scale = hd ** -0.5
    for kh in range(n_kv):
        q = q_ref[0, kh]
        rows = q.shape[0]
        qf = q.astype(F32)
        kseq = jnp.concatenate([kp[p][0, kh] for p in range(n_pages)], axis=0)
        vseq = jnp.concatenate([vp[p][0, kh] for p in range(n_pages)], axis=0)
        km = jnp.concatenate([jnp.mean(kseq[n * MOBA_BLOCK:(n + 1) * MOBA_BLOCK], axis=0, keepdims=True)
                              for n in range(nbp)], axis=0)
        gate = lax.dot_general(q, km.astype(BF16), _NT, preferred_element_type=F32)
        cur = gate
        thr = None
        for a in range(topk):
            thr = jnp.max(cur, axis=1, keepdims=True)
            if a < topk - 1:
                cur = jnp.where(cur == thr, -jnp.inf, cur)
        gbias = jnp.where(gate >= thr, 0.0, NEG)
        s = lax.dot_general(q, kseq.astype(BF16), _NT, preferred_element_type=F32) * scale
        s = jnp.concatenate([s[:, n * MOBA_BLOCK:(n + 1) * MOBA_BLOCK] + gbias[:, n:n + 1]
                             for n in range(nbp)], axis=1)
        kn = kn_ref[0, kh]
        vn = vn_ref[0, kh]
        so = jnp.concatenate([jnp.sum(qf * kn[jn:jn + 1, :], axis=1, keepdims=True) for jn in range(t_len)],
                             axis=1) * scale
        row_t = lax.rem(lax.broadcasted_iota(jnp.int32, (rows, t_len), 0), t_len)
        col = lax.broadcasted_iota(jnp.int32, (rows, t_len), 1)
        so = jnp.where(col <= row_t, so, NEG)
        m = jnp.maximum(jnp.max(s, axis=1, keepdims=True), jnp.max(so, axis=1, keepdims=True))
        p = jnp.exp(s - m)
        po = jnp.exp(so - m)
        l = jnp.sum(p, axis=1, keepdims=True) + jnp.sum(po, axis=1, keepdims=True)
        o = jnp.dot(p.astype(BF16), vseq.astype(BF16), preferred_element_type=F32)
        for jn in range(t_len):
            o = o + po[:, jn:jn + 1] * vn[jn:jn + 1, :]
        o_ref[0, kh] = o * (1.0 / l)


def _moba_sample_call(page_table, q, k_new, v_new, cache_k, cache_v, *, t_len):
    dbsz, n_kv, rows, hd = q.shape
    n_pages = page_table.shape[1]
    page = cache_k.shape[2]
    ppb = MOBA_BLOCK // page
    assert n_pages % ppb == 0, "past length must be a whole number of MoBA blocks"
    nbp = n_pages // ppb
    topk = min(MOBA_TOPK, nbp)
    assert topk > 0
    page_specs = [pl.BlockSpec((1, n_kv, page, hd), functools.partial(lambda b, pt, p: (pt[b, p], 0, 0, 0), p=p))
                  for p in range(n_pages)]
    return pl.pallas_call(
        functools.partial(_moba_sample_kernel, n_pages=n_pages, n_kv=n_kv, t_len=t_len, topk=topk),
        out_shape=jax.ShapeDtypeStruct((dbsz, n_kv, rows, hd), F32),
        grid_spec=pltpu.PrefetchScalarGridSpec(
            num_scalar_prefetch=1,
            grid=(dbsz,),
            in_specs=[pl.BlockSpec((1, n_kv, rows, hd), lambda b, pt: (b, 0, 0, 0)),
                      pl.BlockSpec((1, n_kv, t_len, hd), lambda b, pt: (b, 0, 0, 0)),
                      pl.BlockSpec((1, n_kv, t_len, hd), lambda b, pt: (b, 0, 0, 0))]
                     + page_specs + page_specs,
            out_specs=pl.BlockSpec((1, n_kv, rows, hd), lambda b, pt: (b, 0, 0, 0)),
        ),
        compiler_params=_cparams(("arbitrary",)),
        name="moba_sample",
    )(page_table, q, k_new, v_new, *([cache_k] * n_pages), *([cache_v] * n_pages))


def _rope_tables_t(pos, hd):
    half = (hd // 4) // 2
    inv = ROPE_THETA ** (-jnp.arange(half, dtype=F32) / half)
    ang = pos.astype(F32)[None, :] * inv[:, None]
    return jnp.cos(ang), jnp.sin(ang)


def kernel(x_prompt, x_sample, cache_k, cache_v, page_table, c_prompt, c_sample, ada_w, ada_b, norm1_g, norm2_g, a_w_in, a_b_in, a_g_sgu, a_w_s, a_b_s, a_w_out, kv_norm_g, w_kv, k_norm_g, b_w_q, b_q_norm_g, b_w_o, peer_w_q, peer_sub_keys, peer_u, peer_v):
    bsz, s_len, d = x_prompt.shape
    dbsz, t_len, _ = x_sample.shape
    depth = ada_w.shape[0]
    n_a = a_w_in.shape[0]
    n_kv, page, hd = cache_k.shape[1], cache_k.shape[2], cache_k.shape[3]
    n_heads = b_w_q.shape[-1] // hd
    gqa = n_heads // n_kv
    chunk = a_w_s.shape[-1]
    groups = a_w_s.shape[1]
    gd = a_w_in.shape[-1] // 2 // groups
    tp = bsz * s_len
    tsm = dbsz * t_len
    tile = TOK_TILE
    assert s_len % MOBA_BLOCK == 0 and tsm % PEER_TOK_TILE == 0 and tp % PEER_TOK_TILE == 0
    assert tile % t_len == 0 and tile % chunk == 0
    nb = s_len // MOBA_BLOCK
    past_len = page_table.shape[1] * page

    n_c = bsz + dbsz
    pad = (-n_c) % 8
    c_all = jnp.concatenate([c_prompt, c_sample, jnp.zeros((pad, d), F32)], axis=0)
    mods = _ada_call(c_all, ada_w, ada_b)
    mod_p = [mods[l, :bsz].reshape(bsz, 1, 6 * d) for l in range(depth)]
    mod_s = [jnp.repeat(mods[l, bsz:bsz + dbsz], t_len, axis=0).reshape(tsm // tile, tile, 6 * d)
             for l in range(depth)]
    tiles_per_seq = s_len // tile
    mp_map = lambda i: (i // tiles_per_seq, 0, 0)
    ms_map = lambda i: (i, 0, 0)
    ptiles_per_seq = s_len // PEER_TOK_TILE
    mp_map2 = lambda i, j: (i // ptiles_per_seq, 0, 0)
    mod_s2 = [m.reshape(tsm // PEER_TOK_TILE, PEER_TOK_TILE, 6 * d) for m in mod_s]
    ms_map2 = lambda i, j: (i, 0, 0)

    pos_p = jnp.tile(jnp.arange(s_len, dtype=jnp.int32), bsz)
    pos_s = jnp.tile(past_len + jnp.arange(t_len, dtype=jnp.int32), dbsz)
    cos_p, sin_p = _rope_tables_t(pos_p, hd)
    cos_s, sin_s = _rope_tables_t(pos_s, hd)
    gk_b = jnp.broadcast_to(k_norm_g[:, None], (hd, tile))

    tril = jnp.tril(jnp.ones((chunk, chunk), F32))
    tril_s = jnp.tril(jnp.ones((t_len, t_len), F32))
    eye_s = jnp.eye(tile // t_len, dtype=F32)

    xp = x_prompt.reshape(tp, d)
    xs = x_sample.reshape(tsm, d)
    a_rows = []
    shared = None
    for l in range(depth):
        g1 = norm1_g[l].reshape(1, d)
        g2 = norm2_g[l].reshape(1, d)
        if l < n_a:
            w_in = a_w_in[l].astype(BF16)
            w_out = a_w_out[l].astype(BF16)
            b_in = a_b_in[l].reshape(1, -1)
            gs = a_g_sgu[l].reshape(1, -1)
            wmix_p = (a_w_s[l] * tril).astype(BF16)
            bmix_p = jnp.broadcast_to(a_b_s[l][:, :, None], (groups, chunk, gd))
            ws_s = a_w_s[l][:, :t_len, :t_len] * tril_s
            wmix_s = jnp.einsum('ab,gts->gatbs', eye_s, ws_s).reshape(groups, tile, tile).astype(BF16)
            bmix_s = jnp.broadcast_to(jnp.tile(a_b_s[l][:, :t_len], (1, tile // t_len))[:, :, None],
                                      (groups, tile, gd))
            xp, _ = _gmlp_call(xp, mod_p[l], mp_map, g1, w_in, b_in, gs, wmix_p, bmix_p, w_out,
                               chunk=chunk, emit_v=False)
            xs, v_rows = _gmlp_call(xs, mod_s[l], ms_map, g1, w_in, b_in, gs, wmix_s, bmix_s, w_out,
                                    chunk=tile, emit_v=True)
            a_rows.append(v_rows.reshape(dbsz, t_len, -1))
        else:
            jb = l - n_a
            wq_t = b_w_q[jb].T.astype(BF16)
            wo_t = b_w_o[jb].T.astype(BF16)
            gq_b = jnp.broadcast_to(b_q_norm_g[jb][:, None], (hd, tile))
            qt_p = _qproj_call(xp, mod_p[l], mp_map, g1, wq_t, gq_b, cos_p, sin_p, hd=hd)
            ot_p = _moba_prompt_call(qt_p, shared['kb'], shared['vt'], shared['km'], gqa=gqa)
            xp = _oproj_call(ot_p, wo_t, xp, mod_p[l], mp_map)
            qt_s = _qproj_call(xs, mod_s[l], ms_map, g1, wq_t, gq_b, cos_s, sin_s, hd=hd)
            q_s = qt_s.reshape(n_kv, gqa, hd, dbsz, t_len).transpose(3, 0, 1, 4, 2).reshape(dbsz, n_kv, gqa * t_len, hd)
            o_s = _moba_sample_call(page_table, q_s, shared['k_new'], shared['v_new'], cache_k, cache_v, t_len=t_len)
            ot_s = o_s.reshape(dbsz, n_kv, gqa, t_len, hd).transpose(1, 2, 4, 0, 3).reshape(n_heads * hd, tsm)
            xs = _oproj_call(ot_s.astype(BF16), wo_t, xs, mod_s[l], ms_map)

        wpq_t = peer_w_q[l].T.astype(BF16)
        u_b = peer_u[l].astype(BF16)
        vt_b = peer_v[l].T.astype(BF16)
        hb, ns0, e0, dd, e1 = _peer_select_call(xp, mod_p[l], mp_map, g2, wpq_t, peer_sub_keys[l])
        xp = _peer_dense_call(xp, hb, mod_p[l], mp_map2, u_b, vt_b, ns0, e0, dd, e1)
        hb, ns0, e0, dd, e1 = _peer_select_call(xs, mod_s[l], ms_map, g2, wpq_t, peer_sub_keys[l])
        xs = _peer_dense_call(xs, hb, mod_s2[l], ms_map2, u_b, vt_b, ns0, e0, dd, e1)

        if l == n_a - 1:
            gkv = kv_norm_g.reshape(1, d)
            wkv_t = w_kv.T.astype(BF16)
            kt_p, vt_p, km_p = _kv_call(xp, gkv, wkv_t, gk_b, cos_p, sin_p, n_kv=n_kv, hd=hd)
            kt_s, vt_s, _ = _kv_call(xs, gkv, wkv_t, gk_b, cos_s, sin_s, n_kv=n_kv, hd=hd)
            k5 = kt_p.reshape(n_kv, hd, bsz, nb, MOBA_BLOCK)
            v5 = vt_p.reshape(n_kv, hd, bsz, nb, MOBA_BLOCK)
            shared = {
                'kb': k5.transpose(2, 0, 3, 4, 1).astype(BF16),
                'vt': v5.transpose(2, 0, 3, 1, 4).astype(BF16),
                'km': km_p[:, 0, :].reshape(bsz, nb, n_kv, hd).transpose(0, 2, 1, 3).astype(BF16),
                'k_new': kt_s.reshape(n_kv, hd, dbsz, t_len).transpose(2, 0, 3, 1),
                'v_new': vt_s.reshape(n_kv, hd, dbsz, t_len).transpose(2, 0, 3, 1),
            }
            n_pp = s_len // page
            k_prompt = kt_p.reshape(n_kv, hd, bsz, n_pp, page).transpose(2, 3, 0, 4, 1)
            v_prompt = vt_p.reshape(n_kv, hd, bsz, n_pp, page).transpose(2, 3, 0, 4, 1)

    y_prompt = xp.reshape(bsz, s_len, d)
    y_sample = xs.reshape(dbsz, t_len, d)
    a_v_sample = jnp.stack(a_rows, axis=0)
    return (y_prompt, y_sample, k_prompt, v_prompt, shared['k_new'], shared['v_new'], a_v_sample)
```

```python
import functools
import math

import jax
import jax.numpy as jnp
from jax import lax
from jax.experimental import pallas as pl
from jax.experimental.pallas import tpu as pltpu

F32 = jnp.float32
BF16 = jnp.bfloat16

EPS = 1e-6
MOBA_BLOCK = 256
MOBA_TOPK = 3
ROPE_THETA = 500000.0
PEER_TOPK = 16
NEG = -1e30

V7X_VMEM_BYTES = 64 * 1024 * 1024
VMEM_LIMIT = 56 * 1024 * 1024
LANES = 128

TOK_TILE = 256
PEER_TOK_TILE = 512
PEER_EXPERT_CHUNK = 1024

_NT = (((1,), (1,)), ((), ()))
_TN = (((0,), (0,)), ((), ()))


def _cparams(sem):
    return pltpu.CompilerParams(dimension_semantics=sem, vmem_limit_bytes=VMEM_LIMIT)


def _gelu(x):
    c = math.sqrt(2.0 / math.pi)
    return 0.5 * x * (1.0 + jnp.tanh(c * (x + 0.044715 * (x * x * x))))


def _modnorm(x, g, shift, scale):
    ms = jnp.mean(x * x, axis=-1, keepdims=True)
    return (x * lax.rsqrt(ms + EPS) * g) * (1.0 + scale) + shift


def _head_norm_rope_t(blk, gain, cos, sin, half):
    ms = jnp.mean(blk * blk, axis=0, keepdims=True)
    blk = blk * lax.rsqrt(ms + EPS) * gain
    x1 = blk[0:half]
    x2 = blk[half:2 * half]
    return jnp.concatenate([x1 * cos - x2 * sin, x2 * cos + x1 * sin, blk[2 * half:]], axis=0)


def _ada_kernel(c_ref, w_ref, b_ref, o_ref):
    c = c_ref[...]
    s = c * (1.0 / (1.0 + jnp.exp(-c)))
    o_ref[0] = jnp.dot(s, w_ref[0], preferred_element_type=F32) + b_ref[0]


def _ada_call(c_all, ada_w, ada_b):
    depth, d, n = ada_w.shape
    m = c_all.shape[0]
    tn = 1024
    return pl.pallas_call(
        _ada_kernel,
        out_shape=jax.ShapeDtypeStruct((depth, m, n), F32),
        grid=(depth, n // tn),
        in_specs=[pl.BlockSpec((m, d), lambda l, j: (0, 0)),
                  pl.BlockSpec((1, d, tn), lambda l, j: (l, 0, j)),
                  pl.BlockSpec((1, 1, tn), lambda l, j: (l, 0, j))],
        out_specs=pl.BlockSpec((1, m, tn), lambda l, j: (l, 0, j)),
        compiler_params=_cparams(("arbitrary", "arbitrary")),
        name="ada_mod",
    )(c_all, ada_w, ada_b.reshape(depth, 1, n))


def _gmlp_kernel(x_ref, mod_ref, g1_ref, win_ref, bin_ref, gs_ref, wmix_ref, bmix_ref, wout_ref,
                 *rest, chunk, groups, emit_v):
    if emit_v:
        o_ref, v_ref, vb_scr, y_scr = rest
    else:
        o_ref, vb_scr, y_scr = rest
    x = x_ref[...]
    tt, d = x.shape
    w = vb_scr.shape[1]
    gd = w // groups
    mod = mod_ref[0]
    h = _modnorm(x, g1_ref[...], mod[:, 0:d], mod[:, d:2 * d]).astype(BF16)
    v = _gelu(jnp.dot(h, win_ref[:, w:], preferred_element_type=F32) + bin_ref[:, w:])
    v = v * lax.rsqrt(jnp.mean(v * v, axis=-1, keepdims=True) + EPS) * gs_ref[...]
    if emit_v:
        v_ref[...] = v
    vb_scr[...] = v.astype(BF16)
    for g in range(groups):
        cols = slice(g * gd, (g + 1) * gd)
        u = _gelu(jnp.dot(h, win_ref[:, cols], preferred_element_type=F32) + bin_ref[:, cols])
        wm = wmix_ref[g]
        for c in range(tt // chunk):
            rows = slice(c * chunk, (c + 1) * chunk)
            mix = jnp.dot(wm, vb_scr[rows, cols], preferred_element_type=F32) + bmix_ref[g]
            y_scr[rows, cols] = (u[rows] * mix).astype(BF16)
    m = jnp.dot(y_scr[...], wout_ref[...], preferred_element_type=F32)
    o_ref[...] = x + mod[:, 2 * d:3 * d] * m


def _gmlp_call(x, mod, mod_map, g1, w_in, b_in, g_sgu, wmix, bmix, w_out, *, chunk, emit_v):
    t, d = x.shape
    w2 = w_in.shape[1]
    w = w2 // 2
    groups = wmix.shape[0]
    tt = TOK_TILE
    out_shape = [jax.ShapeDtypeStruct((t, d), F32)]
    out_specs = [pl.BlockSpec((tt, d), lambda i: (i, 0))]
    if emit_v:
        out_shape.append(jax.ShapeDtypeStruct((t, w), F32))
        out_specs.append(pl.BlockSpec((tt, w), lambda i: (i, 0)))
    res = pl.pallas_call(
        functools.partial(_gmlp_kernel, chunk=chunk, groups=groups, emit_v=emit_v),
        out_shape=out_shape,
        grid=(t // tt,),
        in_specs=[pl.BlockSpec((tt, d), lambda i: (i, 0)),
                  pl.BlockSpec((1,) + mod.shape[1:], mod_map),
                  pl.BlockSpec((1, d), lambda i: (0, 0)),
                  pl.BlockSpec((d, w2), lambda i: (0, 0)),
                  pl.BlockSpec((1, w2), lambda i: (0, 0)),
                  pl.BlockSpec((1, w), lambda i: (0, 0)),
                  pl.BlockSpec(wmix.shape, lambda i: (0, 0, 0)),
                  pl.BlockSpec(bmix.shape, lambda i: (0, 0, 0)),
                  pl.BlockSpec((w, d), lambda i: (0, 0))],
        out_specs=out_specs,
        scratch_shapes=[pltpu.VMEM((tt, w), BF16), pltpu.VMEM((tt, w), BF16)],
        compiler_params=_cparams(("arbitrary",)),
        name="gmlp_mixer",
    )(x, mod, g1, w_in, b_in, g_sgu, wmix, bmix, w_out)
    return res if emit_v else (res[0], None)


def _peer_select_kernel(x_ref, mod_ref, g2_ref, wpq_t_ref, keys_ref,
                        hb_ref, ns0_ref, e0_ref, d_ref, e1_ref, q_scr, *, n_heads, topk):
    x = x_ref[...]
    d = x.shape[1]
    mod = mod_ref[0]
    hb = _modnorm(x, g2_ref[...], mod[:, 3 * d:4 * d], mod[:, 4 * d:5 * d]).astype(BF16)
    hb_ref[...] = hb
    q_scr[...] = lax.dot_general(wpq_t_ref[...], hb, _NT, preferred_element_type=F32)
    dk = keys_ref.shape[2]

    def top_values(s, n):
        vals = []
        cur = s
        for a in range(n):
            m = jnp.max(cur, axis=0, keepdims=True)
            vals.append(m)
            if a < n - 1:
                cur = jnp.where(cur == m, -jnp.inf, cur)
        return vals

    n = topk + 1
    inner = 8
    assert n // 2 <= inner and n <= 3 * inner
    pad = [None] * (3 * inner - n)

    def head(hd, carry):
        r0 = pl.multiple_of(hd * (2 * dk), 2 * dk)
        s0 = jnp.dot(keys_ref[0], q_scr[pl.ds(r0, dk), :], preferred_element_type=F32)
        s1 = jnp.dot(keys_ref[1], q_scr[pl.ds(r0 + dk, dk), :], preferred_element_type=F32)
        p0 = top_values(s0, n)
        p1 = top_values(s1, n)
        ninf = jnp.full_like(p0[0], -jnp.inf)
        p1_all = jnp.concatenate(p1 + [ninf for _ in pad], axis=0)
        p0_tail = jnp.concatenate(p0[inner:] + [ninf for _ in range(2 * inner - (n - inner))], axis=0)
        cand = jnp.concatenate([p0[0] + p1_all] + [p0[a] + p1_all[0:inner] for a in range(1, inner)]
                               + [p0_tail + p1[0]], axis=0)
        ms = top_values(cand, n)
        z = jnp.ones_like(ms[0])
        for mk in ms[1:topk]:
            z = z + jnp.exp(mk - ms[0])
        tau = 0.5 * (ms[topk - 1] + ms[topk])
        ns0_ref[hd] = -s0
        e0_ref[hd] = jnp.exp(s0 - p0[0]) * (1.0 / z)
        d_ref[hd] = s1 - tau
        e1_ref[hd] = jnp.exp(s1 - p1[0])
        return carry

    lax.fori_loop(0, n_heads, head, 0)


def _peer_select_call(x, mod, mod_map, g2, wpq_t, sub_keys):
    t, d = x.shape
    nq = wpq_t.shape[0]
    _, n_keys, dk = sub_keys.shape
    n_heads = nq // (2 * dk)
    ts = TOK_TILE
    tab = jax.ShapeDtypeStruct((n_heads, n_keys, t), F32)
    tab_spec = pl.BlockSpec((n_heads, n_keys, ts), lambda i: (0, 0, i))
    return pl.pallas_call(
        functools.partial(_peer_select_kernel, n_heads=n_heads, topk=PEER_TOPK),
        out_shape=[jax.ShapeDtypeStruct((t, d), BF16), tab, tab, tab, tab],
        grid=(t // ts,),
        in_specs=[pl.BlockSpec((ts, d), lambda i: (i, 0)),
                  pl.BlockSpec((1,) + mod.shape[1:], mod_map),
                  pl.BlockSpec((1, d), lambda i: (0, 0)),
                  pl.BlockSpec((nq, d), lambda i: (0, 0)),
                  pl.BlockSpec(sub_keys.shape, lambda i: (0, 0, 0))],
        out_specs=[pl.BlockSpec((ts, d), lambda i: (i, 0)), tab_spec, tab_spec, tab_spec, tab_spec],
        scratch_shapes=[pltpu.VMEM((nq, ts), F32)],
        compiler_params=_cparams(("arbitrary",)),
        name="peer_select",
    )(x, mod, g2, wpq_t, sub_keys)


def _tree_sum(terms):
    while len(terms) > 1:
        terms = [terms[k] + terms[k + 1] for k in range(0, len(terms) - 1, 2)] + (
            [terms[-1]] if len(terms) % 2 else [])
    return terms[0]


def _peer_dense_kernel(hb_ref, u_ref, v_ref, ns0_ref, e0_ref, d_ref, e1_ref, x_ref, mod_ref,
                       o_ref, acc_ref, a_ref, w_ref, *, n_heads, n_keys, n_chunks):
    i = pl.program_id(0)
    j = pl.program_id(1)
    tb, d = x_ref.shape
    ec = u_ref.shape[0]

    @pl.when(jnp.logical_and(i == 0, j == 0))
    def _():
        acc_ref[...] = jnp.zeros_like(acc_ref)
        a_ref[...] = jnp.zeros_like(a_ref)
        w_ref[...] = jnp.zeros_like(w_ref)

    cur = j % 2
    prev = 1 - cur

    part = lax.dot_general(w_ref[cur], v_ref[...], _TN, preferred_element_type=F32)
    acc_ref[...] = jnp.where(j == 2, part, acc_ref[...] + part)

    a_ref[cur] = lax.dot_general(u_ref[...], hb_ref[...], _NT, preferred_element_type=F32)

    sub = 16
    for r0 in range(0, ec, sub):
        r = r0 // n_keys
        j0 = r0 % n_keys
        for ts in range(tb // LANES):
            lanes = slice(ts * LANES, (ts + 1) * LANES)
            terms = []
            for hd in range(n_heads):
                hit = d_ref[hd, j0:j0 + sub, lanes] >= ns0_ref[hd, r:r + 1, lanes]
                val = e1_ref[hd, j0:j0 + sub, lanes] * e0_ref[hd, r:r + 1, lanes]
                terms.append(jnp.where(hit, val, 0.0))
            c = _tree_sum(terms)
            w_ref[prev, r0:r0 + sub, lanes] = (c * _gelu(a_ref[prev, r0:r0 + sub, lanes])).astype(BF16)

    @pl.when(j == n_chunks + 1)
    def _():
        mod = mod_ref[0]
        o_ref[...] = x_ref[...] + mod[:, 5 * d:6 * d] * acc_ref[...]


def _peer_dense_call(x, hb, mod, mod_map2, u_b, v_b, ns0, e0, dd, e1):
    t, d = x.shape
    n_exp = u_b.shape[0]
    n_heads, n_keys, _ = ns0.shape
    tb = min(PEER_TOK_TILE, t)
    ec = min(PEER_EXPERT_CHUNK, n_exp)
    rpc = ec // n_keys
    nc = n_exp // ec
    clamp = lambda c: jnp.clip(c, 0, nc - 1)
    return pl.pallas_call(
        functools.partial(_peer_dense_kernel, n_heads=n_heads, n_keys=n_keys, n_chunks=nc),
        out_shape=jax.ShapeDtypeStruct((t, d), F32),
        grid=(t // tb, nc + 2),
        in_specs=[pl.BlockSpec((tb, d), lambda i, j: (i, 0)),
                  pl.BlockSpec((ec, d), lambda i, j: (clamp(j), 0)),
                  pl.BlockSpec((ec, d), lambda i, j: (clamp(j - 2), 0)),
                  pl.BlockSpec((n_heads, rpc, tb), lambda i, j: (0, clamp(j - 1), i)),
                  pl.BlockSpec((n_heads, rpc, tb), lambda i, j: (0, clamp(j - 1), i)),
                  pl.BlockSpec((n_heads, n_keys, tb), lambda i, j: (0, 0, i)),
                  pl.BlockSpec((n_heads, n_keys, tb), lambda i, j: (0, 0, i)),
                  pl.BlockSpec((tb, d), lambda i, j: (i, 0)),
                  pl.BlockSpec((1,) + mod.shape[1:], mod_map2)],
        out_specs=pl.BlockSpec((tb, d), lambda i, j: (i, 0)),
        scratch_shapes=[pltpu.VMEM((tb, d), F32), pltpu.VMEM((2, ec, tb), F32), pltpu.VMEM((2, ec, tb), BF16)],
        compiler_params=_cparams(("arbitrary", "arbitrary")),
        name="peer_dense",
    )(hb, u_b, v_b, ns0, e0, dd, e1, x, mod)


def _kv_kernel(x_ref, gkv_ref, wkv_t_ref, gk_ref, cos_ref, sin_ref, kt_ref, vt_ref, km_ref, *, n_kv, hd):
    x = x_ref[...]
    tk = x.shape[0]
    xn = x * lax.rsqrt(jnp.mean(x * x, axis=-1, keepdims=True) + EPS) * gkv_ref[...]
    kvt = lax.dot_general(wkv_t_ref[...], xn.astype(BF16), _NT, preferred_element_type=F32)
    cos = cos_ref[...]
    sin = sin_ref[...]
    half = cos.shape[0]
    for h in range(n_kv):
        kt_ref[h * hd:(h + 1) * hd, :] = _head_norm_rope_t(kvt[h * hd:(h + 1) * hd], gk_ref[...], cos, sin, half)
    vt_ref[...] = kvt[n_kv * hd:]
    kt = kt_ref[...]
    hi = kt.astype(BF16)
    lo = (kt - hi.astype(F32)).astype(BF16)
    ones = jnp.ones((8, tk), BF16)
    s = (lax.dot_general(ones, hi, _NT, preferred_element_type=F32)
         + lax.dot_general(ones, lo, _NT, preferred_element_type=F32))
    km_ref[0] = s * (1.0 / tk)


def _kv_call(x, gkv, wkv_t, gk_b, cos_t, sin_t, *, n_kv, hd):
    t, d = x.shape
    tk = TOK_TILE
    assert tk == MOBA_BLOCK
    nkv = n_kv * hd
    half = cos_t.shape[0]
    return pl.pallas_call(
        functools.partial(_kv_kernel, n_kv=n_kv, hd=hd),
        out_shape=[jax.ShapeDtypeStruct((nkv, t), F32), jax.ShapeDtypeStruct((nkv, t), F32),
                   jax.ShapeDtypeStruct((t // tk, 8, nkv), F32)],
        grid=(t // tk,),
        in_specs=[pl.BlockSpec((tk, d), lambda i: (i, 0)),
                  pl.BlockSpec((1, d), lambda i: (0, 0)),
                  pl.BlockSpec((2 * nkv, d), lambda i: (0, 0)),
                  pl.BlockSpec((hd, tk), lambda i: (0, 0)),
                  pl.BlockSpec((half, tk), lambda i: (0, i)),
                  pl.BlockSpec((half, tk), lambda i: (0, i))],
        out_specs=[pl.BlockSpec((nkv, tk), lambda i: (0, i)), pl.BlockSpec((nkv, tk), lambda i: (0, i)),
                   pl.BlockSpec((1, 8, nkv), lambda i: (i, 0, 0))],
        compiler_params=_cparams(("arbitrary",)),
        name="shared_kv",
    )(x, gkv, wkv_t, gk_b, cos_t, sin_t)


def _qproj_kernel(x_ref, mod_ref, g1_ref, wq_t_ref, gq_ref, cos_ref, sin_ref, qt_ref, *, n_heads, hd):
    x = x_ref[...]
    d = x.shape[1]
    mod = mod_ref[0]
    h = _modnorm(x, g1_ref[...], mod[:, 0:d], mod[:, d:2 * d]).astype(BF16)
    qt = lax.dot_general(wq_t_ref[...], h, _NT, preferred_element_type=F32)
    cos = cos_ref[...]
    sin = sin_ref[...]
    half = cos.shape[0]
    for hh in range(n_heads):
        blk = _head_norm_rope_t(qt[hh * hd:(hh + 1) * hd], gq_ref[...], cos, sin, half)
        qt_ref[hh * hd:(hh + 1) * hd, :] = blk.astype(BF16)


def _qproj_call(x, mod, mod_map, g1, wq_t, gq_b, cos_t, sin_t, *, hd):
    t, d = x.shape
    nq = wq_t.shape[0]
    tq = TOK_TILE
    half = cos_t.shape[0]
    return pl.pallas_call(
        functools.partial(_qproj_kernel, n_heads=nq // hd, hd=hd),
        out_shape=jax.ShapeDtypeStruct((nq, t), BF16),
        grid=(t // tq,),
        in_specs=[pl.BlockSpec((tq, d), lambda i: (i, 0)),
                  pl.BlockSpec((1,) + mod.shape[1:], mod_map),
                  pl.BlockSpec((1, d), lambda i: (0, 0)),
                  pl.BlockSpec((nq, d), lambda i: (0, 0)),
                  pl.BlockSpec((hd, tq), lambda i: (0, 0)),
                  pl.BlockSpec((half, tq), lambda i: (0, i)),
                  pl.BlockSpec((half, tq), lambda i: (0, i))],
        out_specs=pl.BlockSpec((nq, tq), lambda i: (0, i)),
        compiler_params=_cparams(("arbitrary",)),
        name="moba_qproj",
    )(x, mod, g1, wq_t, gq_b, cos_t, sin_t)


def _oproj_kernel(ot_ref, wo_t_ref, x_ref, mod_ref, o_ref):
    d = x_ref.shape[1]
    mt = jnp.dot(wo_t_ref[...], ot_ref[...], preferred_element_type=F32)
    mod = mod_ref[0]
    o_ref[...] = x_ref[...] + mod[:, 2 * d:3 * d] * mt.T


def _oproj_call(ot, wo_t, x, mod, mod_map):
    t, d = x.shape
    nq = ot.shape[0]
    tq = TOK_TILE
    return pl.pallas_call(
        _oproj_kernel,
        out_shape=jax.ShapeDtypeStruct((t, d), F32),
        grid=(t // tq,),
        in_specs=[pl.BlockSpec((nq, tq), lambda i: (0, i)),
                  pl.BlockSpec((d, nq), lambda i: (0, 0)),
                  pl.BlockSpec((tq, d), lambda i: (i, 0)),
                  pl.BlockSpec((1,) + mod.shape[1:], mod_map)],
        out_specs=pl.BlockSpec((tq, d), lambda i: (i, 0)),
        compiler_params=_cparams(("arbitrary",)),
        name="moba_oproj",
    )(ot, wo_t, x, mod)


def _moba_prompt_kernel(qt_ref, kb_ref, vt_ref, km_ref, ot_ref, bias_scr, m_scr, l_scr, acc_scr,
                        *, n_kv, gqa, hd, topk):
    qi = pl.program_id(1)
    tq = qt_ref.shape[1]
    nb = kb_ref.shape[2]
    blk = kb_ref.shape[3]
    gt = gqa * tq
    scale = hd ** -0.5
    rr = lax.broadcasted_iota(jnp.int32, (blk, tq), 0)
    tt = lax.broadcasted_iota(jnp.int32, (blk, tq), 1)
    causal = jnp.where(rr <= tt, 0.0, NEG).astype(F32)
    causal = jnp.concatenate([causal] * gqa, axis=1)
    bidx = lax.broadcasted_iota(jnp.int32, (nb, gt), 0)

    def update(st, v_blk):
        m_old = m_scr[...]
        m_new = jnp.maximum(m_old, jnp.max(st, axis=0, keepdims=True))
        p = jnp.exp(st - m_new)
        alpha = jnp.exp(m_old - m_new)
        l_scr[...] = alpha * l_scr[...] + jnp.sum(p, axis=0, keepdims=True)
        acc_scr[...] = alpha * acc_scr[...] + jnp.dot(v_blk, p.astype(BF16), preferred_element_type=F32)
        m_scr[...] = m_new

    for kh in range(n_kv):
        q4 = jnp.concatenate([qt_ref[(kh * gqa + g) * hd:(kh * gqa + g + 1) * hd, :] for g in range(gqa)], axis=1)
        gate = jnp.dot(km_ref[0, kh], q4, preferred_element_type=F32)
        past = bidx < qi
        gm = jnp.where(past, gate, -jnp.inf)
        cur = gm
        thr = None
        for a in range(topk):
            thr = jnp.max(cur, axis=0, keepdims=True)
            if a < topk - 1:
                cur = jnp.where(cur == thr, -jnp.inf, cur)
        bias_scr[...] = jnp.where(jnp.logical_and(past, gm >= thr), 0.0, NEG)
        m_scr[...] = jnp.full_like(m_scr, NEG)
        l_scr[...] = jnp.zeros_like(l_scr)
        acc_scr[...] = jnp.zeros_like(acc_scr)

        def body(n, carry, kh=kh, q4=q4):
            st = jnp.dot(kb_ref[0, kh, n], q4, preferred_element_type=F32) * scale + bias_scr[pl.ds(n, 1), :]
            update(st, vt_ref[0, kh, n])
            return carry

        lax.fori_loop(0, qi, body, 0)
        st = jnp.dot(kb_ref[0, kh, qi], q4, preferred_element_type=F32) * scale + causal
        update(st, vt_ref[0, kh, qi])
        o = acc_scr[...] * (1.0 / l_scr[...])
        for g in range(gqa):
            ot_ref[(kh * gqa + g) * hd:(kh * gqa + g + 1) * hd, :] = o[:, g * tq:(g + 1) * tq].astype(BF16)


def _moba_prompt_call(qt, kb, vt, km, *, gqa):
    nq, t = qt.shape
    bsz, n_kv, nb, blk, hd = kb.shape
    tq = TOK_TILE
    assert tq == blk == MOBA_BLOCK
    topk = min(MOBA_TOPK, nb - 1)
    assert topk > 0
    gt = gqa * tq
    return pl.pallas_call(
        functools.partial(_moba_prompt_kernel, n_kv=n_kv, gqa=gqa, hd=hd, topk=topk),
        out_shape=jax.ShapeDtypeStruct((nq, t), BF16),
        grid=(bsz, nb),
        in_specs=[pl.BlockSpec((nq, tq), lambda b, i: (0, b * nb + i)),
                  pl.BlockSpec((1, n_kv, nb, blk, hd), lambda b, i: (b, 0, 0, 0, 0)),
                  pl.BlockSpec((1, n_kv, nb, hd, blk), lambda b, i: (b, 0, 0, 0, 0)),
                  pl.BlockSpec((1, n_kv, nb, hd), lambda b, i: (b, 0, 0, 0))],
        out_specs=pl.BlockSpec((nq, tq), lambda b, i: (0, b * nb + i)),
        scratch_shapes=[pltpu.VMEM((nb, gt), F32), pltpu.VMEM((1, gt), F32), pltpu.VMEM((1, gt), F32),
                        pltpu.VMEM((hd, gt), F32)],
        compiler_params=_cparams(("arbitrary", "arbitrary")),
        name="moba_prompt",
    )(qt, kb, vt, km)


def _moba_sample_kernel(pt_ref, q_ref, kn_ref, vn_ref, *rest, n_pages, n_kv, t_len, topk):
    kp = rest[:n_pages]
    vp = rest[n_pages:2 * n_pages]
    o_ref = rest[2 * n_pages]
    page = kp[0].shape[2]
    hd = kp[0].shape[3]
    ppb = MOBA_BLOCK // page
    nbp = n_pages // ppb
    scale = hd ** -0.5
    for kh in range(n_kv):
        q = q_ref[0, kh]
        rows = q.shape[0]
        qf = q.astype(F32)
        kseq = jnp.concatenate([kp[p][0, kh] for p in range(n_pages)], axis=0)
        vseq = jnp.concatenate([vp[p][0, kh] for p in range(n_pages)], axis=0)
        km = jnp.concatenate([jnp.mean(kseq[n * MOBA_BLOCK:(n + 1) * MOBA_BLOCK], axis=0, keepdims=True)
                              for n in range(nbp)], axis=0)
        gate = lax.dot_general(q, km.astype(BF16), _NT, preferred_element_type=F32)
        cur = gate
        thr = None
        for a in range(topk):
            thr = jnp.max(cur, axis=1, keepdims=True)
            if a < topk - 1:
                cur = jnp.where(cur == thr, -jnp.inf, cur)
        gbias = jnp.where(gate >= thr, 0.0, NEG)
        s = lax.dot_general(q, kseq.astype(BF16), _NT, preferred_element_type=F32) * scale
        s = jnp.concatenate([s[:, n * MOBA_BLOCK:(n + 1) * MOBA_BLOCK] + gbias[:, n:n + 1]
                             for n in range(nbp)], axis=1)
        kn = kn_ref[0, kh]
        vn = vn_ref[0, kh]
        so = jnp.concatenate([jnp.sum(qf * kn[jn:jn + 1, :], axis=1, keepdims=True) for jn in range(t_len)],
                             axis=1) * scale
        row_t = lax.rem(lax.broadcasted_iota(jnp.int32, (rows, t_len), 0), t_len)
        col = lax.broadcasted_iota(jnp.int32, (rows, t_len), 1)
        so = jnp.where(col <= row_t, so, NEG)
        m = jnp.maximum(jnp.max(s, axis=1, keepdims=True), jnp.max(so, axis=1, keepdims=True))
        p = jnp.exp(s - m)
        po = jnp.exp(so - m)
        l = jnp.sum(p, axis=1, keepdims=True) + jnp.sum(po, axis=1, keepdims=True)
        o = jnp.dot(p.astype(BF16), vseq.astype(BF16), preferred_element_type=F32)
        for jn in range(t_len):
            o = o + po[:, jn:jn + 1] * vn[jn:jn + 1, :]
        o_ref[0, kh] = o * (1.0 / l)


def _moba_sample_call(page_table, q, k_new, v_new, cache_k, cache_v, *, t_len):
    dbsz, n_kv, rows, hd = q.shape
    n_pages = page_table.shape[1]
    page = cache_k.shape[2]
    ppb = MOBA_BLOCK // page
    assert n_pages % ppb == 0, "past length must be a whole number of MoBA blocks"
    nbp = n_pages // ppb
    topk = min(MOBA_TOPK, nbp)
    assert topk > 0
    page_specs = [pl.BlockSpec((1, n_kv, page, hd), functools.partial(lambda b, pt, p: (pt[b, p], 0, 0, 0), p=p))
                  for p in range(n_pages)]
    return pl.pallas_call(
        functools.partial(_moba_sample_kernel, n_pages=n_pages, n_kv=n_kv, t_len=t_len, topk=topk),
        out_shape=jax.ShapeDtypeStruct((dbsz, n_kv, rows, hd), F32),
        grid_spec=pltpu.PrefetchScalarGridSpec(
            num_scalar_prefetch=1,
            grid=(dbsz,),
            in_specs=[pl.BlockSpec((1, n_kv, rows, hd), lambda b, pt: (b, 0, 0, 0)),
                      pl.BlockSpec((1, n_kv, t_len, hd), lambda b, pt: (b, 0, 0, 0)),
                      pl.BlockSpec((1, n_kv, t_len, hd), lambda b, pt: (b, 0, 0, 0))]
                     + page_specs + page_specs,
            out_specs=pl.BlockSpec((1, n_kv, rows, hd), lambda b, pt: (b, 0, 0, 0)),
        ),
        compiler_params=_cparams(("arbitrary",)),
        name="moba_sample",
    )(page_table, q, k_new, v_new, *([cache_k] * n_pages), *([cache_v] * n_pages))


def _rope_tables_t(pos, hd):
    half = (hd // 4) // 2
    inv = ROPE_THETA ** (-jnp.arange(half, dtype=F32) / half)
    ang = pos.astype(F32)[None, :] * inv[:, None]
    return jnp.cos(ang), jnp.sin(ang)


def kernel(x_prompt, x_sample, cache_k, cache_v, page_table, c_prompt, c_sample, ada_w, ada_b, norm1_g, norm2_g, a_w_in, a_b_in, a_g_sgu, a_w_s, a_b_s, a_w_out, kv_norm_g, w_kv, k_norm_g, b_w_q, b_q_norm_g, b_w_o, peer_w_q, peer_sub_keys, peer_u, peer_v):
    bsz, s_len, d = x_prompt.shape
    dbsz, t_len, _ = x_sample.shape
    depth = ada_w.shape[0]
    n_a = a_w_in.shape[0]
    n_kv, page, hd = cache_k.shape[1], cache_k.shape[2], cache_k.shape[3]
    n_heads = b_w_q.shape[-1] // hd
    gqa = n_heads // n_kv
    chunk = a_w_s.shape[-1]
    groups = a_w_s.shape[1]
    gd = a_w_in.shape[-1] // 2 // groups
    tp = bsz * s_len
    tsm = dbsz * t_len
    tile = TOK_TILE
    assert s_len % MOBA_BLOCK == 0 and tsm % PEER_TOK_TILE == 0 and tp % PEER_TOK_TILE == 0
    assert tile % t_len == 0 and tile % chunk == 0
    nb = s_len // MOBA_BLOCK
    past_len = page_table.shape[1] * page

    n_c = bsz + dbsz
    pad = (-n_c) % 8
    c_all = jnp.concatenate([c_prompt, c_sample, jnp.zeros((pad, d), F32)], axis=0)
    mods = _ada_call(c_all, ada_w, ada_b)
    mod_p = [mods[l, :bsz].reshape(bsz, 1, 6 * d) for l in range(depth)]
    mod_s = [jnp.repeat(mods[l, bsz:bsz + dbsz], t_len, axis=0).reshape(tsm // tile, tile, 6 * d)
             for l in range(depth)]
    tiles_per_seq = s_len // tile
    mp_map = lambda i: (i // tiles_per_seq, 0, 0)
    ms_map = lambda i: (i, 0, 0)
    ptiles_per_seq = s_len // PEER_TOK_TILE
    mp_map2 = lambda i, j: (i // ptiles_per_seq, 0, 0)
    mod_s2 = [m.reshape(tsm // PEER_TOK_TILE, PEER_TOK_TILE, 6 * d) for m in mod_s]
    ms_map2 = lambda i, j: (i, 0, 0)

    pos_p = jnp.tile(jnp.arange(s_len, dtype=jnp.int32), bsz)
    pos_s = jnp.tile(past_len + jnp.arange(t_len, dtype=jnp.int32), dbsz)
    cos_p, sin_p = _rope_tables_t(pos_p, hd)
    cos_s, sin_s = _rope_tables_t(pos_s, hd)
    gk_b = jnp.broadcast_to(k_norm_g[:, None], (hd, tile))

    tril = jnp.tril(jnp.ones((chunk, chunk), F32))
    tril_s = jnp.tril(jnp.ones((t_len, t_len), F32))
    eye_s = jnp.eye(tile // t_len, dtype=F32)

    xp = x_prompt.reshape(tp, d)
    xs = x_sample.reshape(tsm, d)
    a_rows = []
    shared = None
    for l in range(depth):
        g1 = norm1_g[l].reshape(1, d)
        g2 = norm2_g[l].reshape(1, d)
        if l < n_a:
            w_in = a_w_in[l].astype(BF16)
            w_out = a_w_out[l].astype(BF16)
            b_in = a_b_in[l].reshape(1, -1)
            gs = a_g_sgu[l].reshape(1, -1)
            wmix_p = (a_w_s[l] * tril).astype(BF16)
            bmix_p = jnp.broadcast_to(a_b_s[l][:, :, None], (groups, chunk, gd))
            ws_s = a_w_s[l][:, :t_len, :t_len] * tril_s
            wmix_s = jnp.einsum('ab,gts->gatbs', eye_s, ws_s).reshape(groups, tile, tile).astype(BF16)
            bmix_s = jnp.broadcast_to(jnp.tile(a_b_s[l][:, :t_len], (1, tile // t_len))[:, :, None],
                                      (groups, tile, gd))
            xp, _ = _gmlp_call(xp, mod_p[l], mp_map, g1, w_in, b_in, gs, wmix_p, bmix_p, w_out,
                               chunk=chunk, emit_v=False)
            xs, v_rows = _gmlp_call(xs, mod_s[l], ms_map, g1, w_in, b_in, gs, wmix_s, bmix_s, w_out,
                                    chunk=tile, emit_v=True)
            a_rows.append(v_rows.reshape(dbsz, t_len, -1))
        else:
            jb = l - n_a
            wq_t = b_w_q[jb].T.astype(BF16)
            wo_t = b_w_o[jb].T.astype(BF16)
            gq_b = jnp.broadcast_to(b_q_norm_g[jb][:, None], (hd, tile))
            qt_p = _qproj_call(xp, mod_p[l], mp_map, g1, wq_t, gq_b, cos_p, sin_p, hd=hd)
            ot_p = _moba_prompt_call(qt_p, shared['kb'], shared['vt'], shared['km'], gqa=gqa)
            xp = _oproj_call(ot_p, wo_t, xp, mod_p[l], mp_map)
            qt_s = _qproj_call(xs, mod_s[l], ms_map, g1, wq_t, gq_b, cos_s, sin_s, hd=hd)
            q_s = qt_s.reshape(n_kv, gqa, hd, dbsz, t_len).transpose(3, 0, 1, 4, 2).reshape(dbsz, n_kv, gqa * t_len, hd)
            o_s = _moba_sample_call(page_table, q_s, shared['k_new'], shared['v_new'], cache_k, cache_v, t_len=t_len)
            ot_s = o_s.reshape(dbsz, n_kv, gqa, t_len, hd).transpose(1, 2, 4, 0, 3).reshape(n_heads * hd, tsm)
            xs = _oproj_call(ot_s.astype(BF16), wo_t, xs, mod_s[l], ms_map)

        wpq_t = peer_w_q[l].T.astype(BF16)
        u_b = peer_u[l].astype(BF16)
        v_b = peer_v[l].astype(BF16)
        hb, ns0, e0, dd, e1 = _peer_select_call(xp, mod_p[l], mp_map, g2, wpq_t, peer_sub_keys[l])
        xp = _peer_dense_call(xp, hb, mod_p[l], mp_map2, u_b, v_b, ns0, e0, dd, e1)
        hb, ns0, e0, dd, e1 = _peer_select_call(xs, mod_s[l], ms_map, g2, wpq_t, peer_sub_keys[l])
        xs = _peer_dense_call(xs, hb, mod_s2[l], ms_map2, u_b, v_b, ns0, e0, dd, e1)

        if l == n_a - 1:
            gkv = kv_norm_g.reshape(1, d)
            wkv_t = w_kv.T.astype(BF16)
            kt_p, vt_p, km_p = _kv_call(xp, gkv, wkv_t, gk_b, cos_p, sin_p, n_kv=n_kv, hd=hd)
            kt_s, vt_s, _ = _kv_call(xs, gkv, wkv_t, gk_b, cos_s, sin_s, n_kv=n_kv, hd=hd)
            k5 = kt_p.reshape(n_kv, hd, bsz, nb, MOBA_BLOCK)
            v5 = vt_p.reshape(n_kv, hd, bsz, nb, MOBA_BLOCK)
            shared = {
                'kb': k5.transpose(2, 0, 3, 4, 1).astype(BF16),
                'vt': v5.transpose(2, 0, 3, 1, 4).astype(BF16),
                'km': km_p[:, 0, :].reshape(bsz, nb, n_kv, hd).transpose(0, 2, 1, 3).astype(BF16),
                'k_new': kt_s.reshape(n_kv, hd, dbsz, t_len).transpose(2, 0, 3, 1),
                'v_new': vt_s.reshape(n_kv, hd, dbsz, t_len).transpose(2, 0, 3, 1),
            }
            n_pp = s_len // page
            k_prompt = kt_p.reshape(n_kv, hd, bsz, n_pp, page).transpose(2, 3, 0, 4, 1)
            v_prompt = vt_p.reshape(n_kv, hd, bsz, n_pp, page).transpose(2, 3, 0, 4, 1)

    y_prompt = xp.reshape(bsz, s_len, d)
    y_sample = xs.reshape(dbsz, t_len, d)
    a_v_sample = jnp.stack(a_rows, axis=0)
    return (y_prompt, y_sample, k_prompt, v_prompt, shared['k_new'], shared['v_new'], a_v_sample)
```
